```python
import math
import jax, jax.numpy as jnp
from jax import lax
import numpy as np

D_MODEL = 1024
BATCH = 8
SEQ = 2048
DEPTH = 2
DEC_BATCH = 128
DEC_SEQ = 1
PAST_LEN = 16384
PAGE_SIZE = 128

N_MIXERS = 2
N_S5_LAYERS = (DEPTH + 1) // 2
N_CONV_LAYERS = DEPTH // 2
S5_WIDTH = D_MODEL
S5_GROUP_CH = 16
S5_GROUPS = S5_WIDTH // S5_GROUP_CH
S5_STATE = 64
D_CONV = D_MODEL
CONV_W = 3
D_FF = 2816
ALPHA = (2.0 * DEPTH) ** 0.25
BETA = (8.0 * DEPTH) ** -0.25
LN_EPS = 1e-5

kernel_name = "hybrid_s5_shortconv_convffn_decode_step"


def layer_norm(x, g, b):
    xf = x.astype(jnp.float32)
    mu = jnp.mean(xf, axis=-1, keepdims=True)
    var = jnp.mean(jnp.square(xf - mu), axis=-1, keepdims=True)
    return ((xf - mu) * lax.rsqrt(var + LN_EPS) * g.astype(jnp.float32) + b.astype(jnp.float32)).astype(x.dtype)


def modulate(x, shift, scale):
    return x * (1.0 + scale[:, None, :]) + shift[:, None, :]


def causal_dwconv(v, buf, w):
    seq = v.shape[1]
    cat = jnp.concatenate([buf.astype(v.dtype), v], axis=1)
    out = sum(w[k] * cat[:, k:k + seq] for k in range(CONV_W))
    return out, cat[:, -(CONV_W - 1):]


def _lin_combine(e1, e2):
    a1, b1 = e1
    a2, b2 = e2
    return a2 * a1, a2 * b1 + b2


def s5_mixer(h, h0_re, h0_im, w_in, lam_re, lam_im, log_dt, b_re, b_im, c_re, c_im, d_skip, w_glu):
    n, seq, _ = h.shape
    f32 = jnp.float32
    u = (h @ w_in).astype(f32).reshape(n, seq, S5_GROUPS, S5_GROUP_CH)
    lam = lax.complex(lam_re.astype(f32), lam_im.astype(f32))
    dt = jnp.exp(log_dt.astype(f32))[:, None]
    a_bar = jnp.exp(lam * dt)
    b_bar = ((a_bar - 1.0) / lam)[..., None] * lax.complex(b_re.astype(f32), b_im.astype(f32))
    bu = jnp.einsum('nlgc,gpc->nlgp', u.astype(jnp.complex64), b_bar)
    h0 = lax.complex(h0_re.astype(f32), h0_im.astype(f32))
    bu = bu.at[:, 0].add(a_bar * h0)
    a_seq = jnp.broadcast_to(a_bar, bu.shape)
    _, states = lax.associative_scan(_lin_combine, (a_seq, bu), axis=1)
    y = (jnp.einsum('nlgp,gcp->nlgc', states.real, c_re.astype(f32))
         - jnp.einsum('nlgp,gcp->nlgc', states.imag, c_im.astype(f32)))
    y = (y + d_skip.astype(f32).reshape(S5_GROUPS, S5_GROUP_CH) * u).reshape(n, seq, S5_WIDTH)
    y = jax.nn.gelu(y).astype(h.dtype)
    z = y @ w_glu
    out = z[..., :D_MODEL] * jax.nn.sigmoid(z[..., D_MODEL:])
    last = states[:, -1]
    return out, last.real, last.imag


def short_conv_mixer(h, buf, w_in, conv_w, w_out):
    z = h @ w_in
    gate_b, gate_c, v = z[..., :D_CONV], z[..., D_CONV:2 * D_CONV], z[..., 2 * D_CONV:]
    conv_out, new_buf = causal_dwconv(gate_c * v, buf, conv_w)
    return (gate_b * conv_out) @ w_out, new_buf


def conv_ffn(h, buf, w_up, conv_w, conv_b, w_down):
    up = h @ w_up
    conv_out, new_buf = causal_dwconv(up, buf, conv_w)
    conv_out = conv_out + conv_b
    val, gate = conv_out[..., :D_FF], conv_out[..., D_FF:]
    return (jax.nn.gelu(gate) * val) @ w_down, new_buf


def trunk(x, c, s5_re, s5_im, conv_buf, ffn_buf, p):
    n = c.shape[0]
    mods = (jax.nn.silu(c) @ p['w_ada'] + p['b_ada']).reshape(n, DEPTH, 2, 3, D_MODEL)
    new_re, new_im, new_conv, new_ffn = [], [], [], []
    for i in range(DEPTH):
        shift, scale, gate = mods[:, i, 0, 0], mods[:, i, 0, 1], mods[:, i, 0, 2]
        h = modulate(x, shift, scale)
        if i % N_MIXERS == 0:
            a = i // N_MIXERS
            m, st_re, st_im = s5_mixer(h, s5_re[a], s5_im[a], p['s5_w_in'][a], p['s5_lam_re'][a], p['s5_lam_im'][a],
                                       p['s5_log_dt'][a], p['s5_b_re'][a], p['s5_b_im'][a], p['s5_c_re'][a],
                                       p['s5_c_im'][a], p['s5_d'][a], p['s5_w_glu'][a])
            new_re.append(st_re)
            new_im.append(st_im)
        else:
            b = i // N_MIXERS
            m, nb = short_conv_mixer(h, conv_buf[b], p['sc_w_in'][b], p['sc_conv_w'][b], p['sc_w_out'][b])
            new_conv.append(nb)
        x = layer_norm(ALPHA * x + gate[:, None, :] * m, p['ln_g'][i, 0], p['ln_b'][i, 0])
        shift, scale, gate = mods[:, i, 1, 0], mods[:, i, 1, 1], mods[:, i, 1, 2]
        h = modulate(x, shift, scale)
        f, fb = conv_ffn(h, ffn_buf[i], p['ffn_w_up'][i], p['ffn_conv_w'][i], p['ffn_conv_b'][i], p['ffn_w_down'][i])
        new_ffn.append(fb)
        x = layer_norm(ALPHA * x + gate[:, None, :] * f, p['ln_g'][i, 1], p['ln_b'][i, 1])
    return x, jnp.stack(new_re), jnp.stack(new_im), jnp.stack(new_conv), jnp.stack(new_ffn)


def setup_inputs(seed: int = 0) -> dict:
    key = jax.random.key(seed)
    ks = jax.random.split(key, 32)
    f32 = jnp.float32

    def nrm(k, shape, std):
        return (jax.random.normal(k, shape, f32) * std).astype(f32)

    d = D_MODEL
    lam_im = (jnp.pi * jnp.arange(S5_STATE, dtype=f32))[None, None, :] + nrm(ks[10], (N_S5_LAYERS, S5_GROUPS, S5_STATE), 0.01)
    sc_w_in = nrm(ks[20], (N_CONV_LAYERS, d, 3 * D_CONV), d ** -0.5)
    sc_w_in = sc_w_in * jnp.concatenate([jnp.ones((2 * D_CONV,), f32), jnp.full((D_CONV,), BETA, f32)])
    ffn_w_up = nrm(ks[23], (DEPTH, d, 2 * D_FF), d ** -0.5)
    ffn_w_up = ffn_w_up * jnp.concatenate([jnp.full((D_FF,), BETA, f32), jnp.ones((D_FF,), f32)])
    s5_w_glu = nrm(ks[17], (N_S5_LAYERS, S5_WIDTH, 2 * d), S5_WIDTH ** -0.5)
    s5_w_glu = s5_w_glu * jnp.concatenate([jnp.full((d,), BETA, f32), jnp.ones((d,), f32)])
    return {
        'x_prompt': nrm(ks[0], (BATCH, SEQ, d), 1.0),
        'x_sample': nrm(ks[1], (DEC_BATCH, DEC_SEQ, d), 1.0),
        'c_prompt': nrm(ks[2], (BATCH, d), 1.0),
        'c_sample': nrm(ks[3], (DEC_BATCH, d), 1.0),
        'state_s5_re': nrm(ks[4], (N_S5_LAYERS, DEC_BATCH, S5_GROUPS, S5_STATE), 0.5),
        'state_s5_im': nrm(ks[5], (N_S5_LAYERS, DEC_BATCH, S5_GROUPS, S5_STATE), 0.5),
        'state_conv': nrm(ks[6], (N_CONV_LAYERS, DEC_BATCH, CONV_W - 1, D_CONV), 1.0),
        'state_ffn': nrm(ks[7], (DEPTH, DEC_BATCH, CONV_W - 1, 2 * D_FF), 0.7),
        'w_ada': nrm(ks[8], (d, DEPTH * 2 * 3 * d), d ** -0.5),
        'b_ada': nrm(ks[9], (DEPTH * 2 * 3 * d,), 0.02),
        's5_w_in': nrm(ks[11], (N_S5_LAYERS, d, S5_WIDTH), BETA * d ** -0.5),
        's5_lam_re': -0.5 + nrm(ks[12], (N_S5_LAYERS, S5_GROUPS, S5_STATE), 0.01),
        's5_lam_im': lam_im,
        's5_log_dt': jax.random.uniform(ks[13], (N_S5_LAYERS, S5_GROUPS), f32, math.log(1e-3), math.log(1e-1)),
        's5_b_re': nrm(ks[14], (N_S5_LAYERS, S5_GROUPS, S5_STATE, S5_GROUP_CH), (2.0 * S5_GROUP_CH) ** -0.5),
        's5_b_im': nrm(ks[15], (N_S5_LAYERS, S5_GROUPS, S5_STATE, S5_GROUP_CH), (2.0 * S5_GROUP_CH) ** -0.5),
        's5_c_re': nrm(ks[16], (N_S5_LAYERS, S5_GROUPS, S5_GROUP_CH, S5_STATE), (2.0 * S5_STATE) ** -0.5),
        's5_c_im': nrm(ks[18], (N_S5_LAYERS, S5_GROUPS, S5_GROUP_CH, S5_STATE), (2.0 * S5_STATE) ** -0.5),
        's5_d': nrm(ks[19], (N_S5_LAYERS, S5_WIDTH), 1.0),
        's5_w_glu': s5_w_glu,
        'sc_w_in': sc_w_in,
        'sc_conv_w': nrm(ks[21], (N_CONV_LAYERS, CONV_W, D_CONV), CONV_W ** -0.5),
        'sc_w_out': nrm(ks[22], (N_CONV_LAYERS, D_CONV, d), BETA * D_CONV ** -0.5),
        'ffn_w_up': ffn_w_up,
        'ffn_conv_w': nrm(ks[24], (DEPTH, CONV_W, 2 * D_FF), CONV_W ** -0.5),
        'ffn_conv_b': nrm(ks[25], (DEPTH, 2 * D_FF), 0.02),
        'ffn_w_down': nrm(ks[26], (DEPTH, D_FF, d), BETA * D_FF ** -0.5),
        'ln_g': 1.0 + nrm(ks[27], (DEPTH, 2, d), 0.02),
        'ln_b': nrm(ks[28], (DEPTH, 2, d), 0.02),
    }


def reference(x_prompt, x_sample, c_prompt, c_sample, state_s5_re, state_s5_im, state_conv, state_ffn,
              w_ada, b_ada, s5_w_in, s5_lam_re, s5_lam_im, s5_log_dt, s5_b_re, s5_b_im, s5_c_re, s5_c_im,
              s5_d, s5_w_glu, sc_w_in, sc_conv_w, sc_w_out, ffn_w_up, ffn_conv_w, ffn_conv_b, ffn_w_down,
              ln_g, ln_b):
    p = {'w_ada': w_ada, 'b_ada': b_ada, 's5_w_in': s5_w_in, 's5_lam_re': s5_lam_re, 's5_lam_im': s5_lam_im,
         's5_log_dt': s5_log_dt, 's5_b_re': s5_b_re, 's5_b_im': s5_b_im, 's5_c_re': s5_c_re, 's5_c_im': s5_c_im,
         's5_d': s5_d, 's5_w_glu': s5_w_glu, 'sc_w_in': sc_w_in, 'sc_conv_w': sc_conv_w, 'sc_w_out': sc_w_out,
         'ffn_w_up': ffn_w_up, 'ffn_conv_w': ffn_conv_w, 'ffn_conv_b': ffn_conv_b, 'ffn_w_down': ffn_w_down,
         'ln_g': ln_g, 'ln_b': ln_b}
    nb = x_prompt.shape[0]
    z_s5 = jnp.zeros((N_S5_LAYERS, nb, S5_GROUPS, S5_STATE), jnp.float32)
    z_conv = jnp.zeros((N_CONV_LAYERS, nb, CONV_W - 1, D_CONV), x_prompt.dtype)
    z_ffn = jnp.zeros((DEPTH, nb, CONV_W - 1, 2 * D_FF), x_prompt.dtype)
    y_prompt, s5re_p, s5im_p, conv_p, ffn_p = trunk(x_prompt, c_prompt, z_s5, z_s5, z_conv, z_ffn, p)
    y_sample, s5re_s, s5im_s, conv_s, ffn_s = trunk(x_sample, c_sample, state_s5_re, state_s5_im,
                                                    state_conv, state_ffn, p)
    return (y_prompt, y_sample, s5re_p, s5im_p, s5re_s, s5im_s, conv_p, conv_s, ffn_p, ffn_s)
```

```python
import functools

import jax
import jax.numpy as jnp
from jax import lax
from jax.experimental import pallas as pl
from jax.experimental.pallas import tpu as pltpu

F32 = jnp.float32
BF16 = jnp.bfloat16

D_MODEL = 1024
N_PROMPT = 8
SEQ = 2048
N_LAYERS = 2
N_SAMPLE = 128
S5_GROUP_CH = 16
S5_GROUPS = D_MODEL // S5_GROUP_CH
S5_STATE = 64
CONV_TAPS = 3
D_FF = 2816
RES_ALPHA = (2.0 * N_LAYERS) ** 0.25
LN_EPS = 1e-5

LANES = 128
SUBLANES = 8
S5_T = SUBLANES
S5_NCH = SEQ // S5_T
S5_GT = LANES // S5_GROUP_CH
S5_NJ = S5_GROUPS // S5_GT
S5_SW = S5_GT * S5_STATE
S5_CW = S5_T * LANES
FF_CHUNK = 256
ROW_TILE = 512
S5_ROW_TILE = 1024
VMEM_LIMIT = 56 * 1024 * 1024


def _params(n_grid_dims):
    return pltpu.CompilerParams(
        dimension_semantics=("arbitrary",) * n_grid_dims,
        vmem_limit_bytes=VMEM_LIMIT,
    )


def _sigmoid(v):
    return 1.0 / (1.0 + jnp.exp(-v))


def _layer_norm(v, g, b):
    mu = jnp.mean(v, axis=-1, keepdims=True)
    d = v - mu
    var = jnp.mean(d * d, axis=-1, keepdims=True)
    return d * lax.rsqrt(var + LN_EPS) * g + b


def _mm(a, b):
    return jnp.dot(a, b, preferred_element_type=F32)


def _ada_kernel(c_ref, w_ref, b_ref, o_ref):
    c = c_ref[...]
    s = (c * _sigmoid(c)).astype(BF16)
    o_ref[...] = _mm(s, w_ref[...].astype(BF16)) + b_ref[...]


def _ada(c_all, w_ada, b_ada):
    n = c_all.shape[0]
    width = w_ada.shape[1]
    blk = 1024
    return pl.pallas_call(
        _ada_kernel,
        grid=(width // blk,),
        in_specs=[
            pl.BlockSpec((n, D_MODEL), lambda i: (0, 0)),
            pl.BlockSpec((D_MODEL, blk), lambda i: (0, i)),
            pl.BlockSpec((1, blk), lambda i: (0, i)),
        ],
        out_specs=pl.BlockSpec((n, blk), lambda i: (0, i)),
        out_shape=jax.ShapeDtypeStruct((n, width), F32),
        compiler_params=_params(1),
        name="ada_mod",
    )(c_all, w_ada, b_ada.reshape(1, width))


def _s5_prep_kernel(lre_ref, lim_ref, ldt_ref, bre_ref, bim_ref, cre_ref, cim_ref,
                    m_ref, e_ref, f_ref, f0_ref, a1_ref, at_ref):
    lre = lre_ref[0]
    lim = lim_ref[0]
    dt = jnp.exp(ldt_ref[0])
    rows = lax.broadcasted_iota(jnp.int32, (LANES, S5_SW), 0) // S5_GROUP_CH
    cols = lax.broadcasted_iota(jnp.int32, (LANES, S5_SW), 1) // S5_STATE
    same_group = rows == cols

    def a_pow(e):
        mag = jnp.exp(lre * dt * float(e))
        ang = lim * dt * float(e)
        return mag * jnp.cos(ang), mag * jnp.sin(ang)

    a1r, a1i = a_pow(1)
    nr, ni = a1r - 1.0, a1i
    den = lre * lre + lim * lim
    kr = (nr * lre + ni * lim) / den
    ki = (ni * lre - nr * lim) / den
    bre, bim = bre_ref[0], bim_ref[0]
    bbr = jnp.where(same_group, kr * bre - ki * bim, 0.0)
    bbi = jnp.where(same_group, kr * bim + ki * bre, 0.0)
    cre = jnp.where(same_group, cre_ref[0], 0.0)
    cim = jnp.where(same_group, cim_ref[0], 0.0)

    def e_blk(e):
        ar, ai = a_pow(e)
        return jnp.concatenate([ar * bbr - ai * bbi, ar * bbi + ai * bbr], axis=1)

    def ft_blk(e):
        ar, ai = a_pow(e)
        return jnp.concatenate([cre * ar - cim * ai, -(cre * ai + cim * ar)], axis=1)

    ft0 = ft_blk(0)
    e_blks = [e_blk(e) for e in range(S5_T)]
    k_blks = [
        lax.dot_general(eb, ft0, (((1,), (1,)), ((), ())),
                        precision=lax.Precision.HIGHEST, preferred_element_type=F32)
        for eb in e_blks
    ]
    zero = jnp.zeros((LANES, LANES), F32)
    for s in range(S5_T):
        e_ref[0, s * LANES:(s + 1) * LANES, :] = e_blks[S5_T - 1 - s].astype(BF16)
        for t in range(S5_T):
            blk = k_blks[t - s] if t >= s else zero
            m_ref[0, s * LANES:(s + 1) * LANES, t * LANES:(t + 1) * LANES] = blk.astype(BF16)
    for t in range(S5_T):
        f_ref[0, :, t * LANES:(t + 1) * LANES] = ft_blk(t + 1).T.astype(BF16)
    f0_ref[0] = ft0.T.astype(BF16)
    a1_ref[0] = jnp.concatenate([a1r, a1i], axis=1)
    atr, ati = a_pow(S5_T)
    at_ref[0] = jnp.concatenate([atr, ati], axis=1)


def _s5_prep(lam_re, lam_im, log_dt, b_re, b_im, c_re, c_im):
    nj, gt, p, ch = S5_NJ, S5_GT, S5_STATE, S5_GROUP_CH

    def lane_vec(v):
        return v.reshape(nj, 1, gt * p)

    def b_rows(b):
        t = b.reshape(nj, gt, p, ch).transpose(0, 3, 1, 2).reshape(nj, 1, ch, gt * p)
        return jnp.broadcast_to(t, (nj, gt, ch, gt * p)).reshape(nj, gt * ch, gt * p)

    def c_rows(c):
        t = c.reshape(nj, gt, ch, p).transpose(0, 2, 1, 3).reshape(nj, 1, ch, gt * p)
        return jnp.broadcast_to(t, (nj, gt, ch, gt * p)).reshape(nj, gt * ch, gt * p)

    ldt = jnp.broadcast_to(log_dt[:, None], (S5_GROUPS, p))
    vec_spec = pl.BlockSpec((1, 1, S5_SW), lambda j: (j, 0, 0))
    mat_spec = pl.BlockSpec((1, LANES, S5_SW), lambda j: (j, 0, 0))
    big_spec = pl.BlockSpec((1, S5_CW, S5_CW), lambda j: (j, 0, 0))
    sw2 = 2 * S5_SW
    return pl.pallas_call(
        _s5_prep_kernel,
        grid=(nj,),
        in_specs=[vec_spec, vec_spec, vec_spec, mat_spec, mat_spec, mat_spec, mat_spec],
        out_specs=[
            big_spec,
            pl.BlockSpec((1, S5_CW, sw2), lambda j: (j, 0, 0)),
            pl.BlockSpec((1, sw2, S5_CW), lambda j: (j, 0, 0)),
            pl.BlockSpec((1, sw2, LANES), lambda j: (j, 0, 0)),
            pl.BlockSpec((1, 1, sw2), lambda j: (j, 0, 0)),
            pl.BlockSpec((1, 1, sw2), lambda j: (j, 0, 0)),
        ],
        out_shape=[
            jax.ShapeDtypeStruct((nj, S5_CW, S5_CW), BF16),
            jax.ShapeDtypeStruct((nj, S5_CW, sw2), BF16),
            jax.ShapeDtypeStruct((nj, sw2, S5_CW), BF16),
            jax.ShapeDtypeStruct((nj, sw2, LANES), BF16),
            jax.ShapeDtypeStruct((nj, 1, sw2), F32),
            jax.ShapeDtypeStruct((nj, 1, sw2), F32),
        ],
        compiler_params=_params(1),
        name="s5_prep",
    )(lane_vec(lam_re), lane_vec(lam_im), lane_vec(ldt),
      b_rows(b_re), b_rows(b_im), c_rows(c_re), c_rows(c_im))


def _s5_in_kernel(x_ref, mod_ref, w_ref, u_ref, s_ref):
    mod = mod_ref[0]
    shift, scale = mod[:, 0:D_MODEL], mod[:, D_MODEL:2 * D_MODEL]
    h = (x_ref[0] * (1.0 + scale) + shift).astype(BF16)
    u = _mm(h, w_ref[...])
    for j in range(S5_NJ):
        s_ref[j] = u[:, j * LANES:(j + 1) * LANES]
    n_chunks = x_ref.shape[1] // S5_T
    for j in range(S5_NJ):
        for t in range(S5_T):
            piece = s_ref[j, pl.ds(t, n_chunks, stride=S5_T), :]
            u_ref[j, :, t * LANES:(t + 1) * LANES] = piece.astype(BF16)


def _s5_in(x, mod, w_in):
    tiles = SEQ // S5_ROW_TILE
    ch_rows = S5_ROW_TILE // S5_T
    return pl.pallas_call(
        _s5_in_kernel,
        grid=(N_PROMPT, tiles),
        in_specs=[
            pl.BlockSpec((1, S5_ROW_TILE, D_MODEL), lambda n, l: (n, l, 0)),
            pl.BlockSpec((1, 1, 3 * D_MODEL), lambda n, l: (n, 0, 0)),
            pl.BlockSpec((D_MODEL, D_MODEL), lambda n, l: (0, 0)),
        ],
        out_specs=pl.BlockSpec((S5_NJ, ch_rows, S5_CW), lambda n, l: (0, n * tiles + l, 0)),
        out_shape=jax.ShapeDtypeStruct((S5_NJ, N_PROMPT * S5_NCH, S5_CW), BF16),
        scratch_shapes=[pltpu.VMEM((S5_NJ, S5_ROW_TILE, LANES), F32)],
        compiler_params=_params(2),
        name="s5_in",
    )(x, mod, w_in)


def _s5_scan_kernel(u_ref, m_ref, e_ref, f_ref, at_ref, d_ref, y_ref, st_ref, s_ref):
    n_tiles = 2 * S5_SW // LANES
    half = n_tiles // 2
    for n in range(N_PROMPT):
        rows = slice(n * S5_NCH, (n + 1) * S5_NCH)
        s_loc = _mm(u_ref[0, rows, :], e_ref[0])
        for q in range(n_tiles):
            s_ref[q, rows, :] = s_loc[:, q * LANES:(q + 1) * LANES]

    a = [jnp.broadcast_to(at_ref[0, :, q * LANES:(q + 1) * LANES], (N_PROMPT, LANES))
         for q in range(n_tiles)]

    def step(k, carry):
        loc = [s_ref[q, pl.ds(k, N_PROMPT, stride=S5_NCH), :] for q in range(n_tiles)]
        for q in range(n_tiles):
            s_ref[q, pl.ds(k, N_PROMPT, stride=S5_NCH), :] = carry[q]
        new_re = [a[q] * carry[q] - a[half + q] * carry[half + q] + loc[q] for q in range(half)]
        new_im = [a[q] * carry[half + q] + a[half + q] * carry[q] + loc[half + q]
                  for q in range(half)]
        return tuple(new_re + new_im)

    zero = jnp.zeros((N_PROMPT, LANES), F32)
    final = lax.fori_loop(0, S5_NCH, step, (zero,) * n_tiles)
    st_ref[0] = jnp.concatenate(final, axis=1)

    d = d_ref[0]
    for n in range(N_PROMPT):
        rows = slice(n * S5_NCH, (n + 1) * S5_NCH)
        u = u_ref[0, rows, :]
        s_in = jnp.concatenate([s_ref[q, rows, :] for q in range(n_tiles)], axis=1).astype(BF16)
        y = _mm(u, m_ref[0]) + _mm(s_in, f_ref[0]) + d * u.astype(F32)
        y_ref[0, rows, :] = jax.nn.gelu(y).astype(BF16)


def _s5_scan(u, m_mat, e_mat, f_mat, a_t, d_tiled):
    rows = N_PROMPT * S5_NCH
    sw2 = 2 * S5_SW
    return pl.pallas_call(
        _s5_scan_kernel,
        grid=(S5_NJ,),
        in_specs=[
            pl.BlockSpec((1, rows, S5_CW), lambda j: (j, 0, 0)),
            pl.BlockSpec((1, S5_CW, S5_CW), lambda j: (j, 0, 0)),
            pl.BlockSpec((1, S5_CW, sw2), lambda j: (j, 0, 0)),
            pl.BlockSpec((1, sw2, S5_CW), lambda j: (j, 0, 0)),
            pl.BlockSpec((1, 1, sw2), lambda j: (j, 0, 0)),
            pl.BlockSpec((1, 1, S5_CW), lambda j: (j, 0, 0)),
        ],
        out_specs=[
            pl.BlockSpec((1, rows, S5_CW), lambda j: (j, 0, 0)),
            pl.BlockSpec((1, N_PROMPT, sw2), lambda j: (j, 0, 0)),
        ],
        out_shape=[
            jax.ShapeDtypeStruct((S5_NJ, rows, S5_CW), BF16),
            jax.ShapeDtypeStruct((S5_NJ, N_PROMPT, sw2), F32),
        ],
        scratch_shapes=[pltpu.VMEM((sw2 // LANES, rows, LANES), F32)],
        compiler_params=_params(1),
        name="s5_scan",
    )(u, m_mat, e_mat, f_mat, a_t, d_tiled)


def _s5_out_kernel(y_ref, x_ref, mod_ref, w_ref, g_ref, b_ref, o_ref, s_ref):
    gate = mod_ref[0][:, 2 * D_MODEL:3 * D_MODEL]
    n_chunks = x_ref.shape[1] // S5_T
    w = w_ref[...]
    for t in range(S5_T):
        y = jnp.concatenate(
            [y_ref[j, :, t * LANES:(t + 1) * LANES] for j in range(S5_NJ)], axis=1)
        z = _mm(y, w)
        m = z[:, 0:D_MODEL] * _sigmoid(z[:, D_MODEL:2 * D_MODEL])
        for q in range(D_MODEL // LANES):
            s_ref[q, pl.ds(t, n_chunks, stride=S5_T), :] = m[:, q * LANES:(q + 1) * LANES]
    m_all = jnp.concatenate([s_ref[q] for q in range(D_MODEL // LANES)], axis=1)
    o_ref[0] = _layer_norm(RES_ALPHA * x_ref[0] + gate * m_all, g_ref[...], b_ref[...])


def _s5_out(y, x, mod, w_glu, ln_g, ln_b):
    tiles = SEQ // S5_ROW_TILE
    ch_rows = S5_ROW_TILE // S5_T
    return pl.pallas_call(
        _s5_out_kernel,
        grid=(N_PROMPT, tiles),
        in_specs=[
            pl.BlockSpec((S5_NJ, ch_rows, S5_CW), lambda n, l: (0, n * tiles + l, 0)),
            pl.BlockSpec((1, S5_ROW_TILE, D_MODEL), lambda n, l: (n, l, 0)),
            pl.BlockSpec((1, 1, 3 * D_MODEL), lambda n, l: (n, 0, 0)),
            pl.BlockSpec((D_MODEL, 2 * D_MODEL), lambda n, l: (0, 0)),
            pl.BlockSpec((1, D_MODEL), lambda n, l: (0, 0)),
            pl.BlockSpec((1, D_MODEL), lambda n, l: (0, 0)),
        ],
        out_specs=pl.BlockSpec((1, S5_ROW_TILE, D_MODEL), lambda n, l: (n, l, 0)),
        out_shape=jax.ShapeDtypeStruct((N_PROMPT, SEQ, D_MODEL), F32),
        scratch_shapes=[pltpu.VMEM((D_MODEL // LANES, S5_ROW_TILE, LANES), F32)],
        compiler_params=_params(2),
        name="s5_out",
    )(y, x, mod, w_glu, ln_g, ln_b)


def _causal_conv_tile(v, carry_ref, cw_ref, c0, width, row_ids):
    rows = v.shape[0]
    prev = carry_ref[:, c0:c0 + width]
    p1 = prev[SUBLANES - 1:SUBLANES, :]
    p2 = prev[SUBLANES - 2:SUBLANES - 1, :]
    s1 = jnp.where(row_ids == 0, p1, pltpu.roll(v, 1, 0))
    s2 = jnp.where(row_ids == 0, p2, jnp.where(row_ids == 1, p1, pltpu.roll(v, 2, 0)))
    carry_ref[:, c0:c0 + width] = v[rows - SUBLANES:rows, :]
    w0 = cw_ref[0:1, c0:c0 + width]
    w1 = cw_ref[1:2, c0:c0 + width]
    w2 = cw_ref[2:3, c0:c0 + width]
    return w0 * s2 + w1 * s1 + w2 * v


def _ffn_kernel(x_ref, mod_ref, wup_ref, cw_ref, cb_ref, wdn_ref, g_ref, b_ref,
                o_ref, st_ref, carry_ref, acc_ref):
    l = pl.program_id(1)

    @pl.when(l == 0)
    def _():
        carry_ref[...] = jnp.zeros_like(carry_ref)

    mod = mod_ref[0]
    shift, scale, gate = (mod[:, 0:D_MODEL], mod[:, D_MODEL:2 * D_MODEL],
                          mod[:, 2 * D_MODEL:3 * D_MODEL])
    x = x_ref[0]
    h = (x * (1.0 + scale) + shift).astype(BF16)
    row_ids = lax.broadcasted_iota(jnp.int32, (x.shape[0], FF_CHUNK), 0)
    for c in range(D_FF // FF_CHUNK):
        cv, cg = c * FF_CHUNK, D_FF + c * FF_CHUNK
        up_v = _mm(h, wup_ref[:, cv:cv + FF_CHUNK])
        up_g = _mm(h, wup_ref[:, cg:cg + FF_CHUNK])
        val = _causal_conv_tile(up_v, carry_ref, cw_ref, cv, FF_CHUNK, row_ids) \
            + cb_ref[:, cv:cv + FF_CHUNK]
        gat = _causal_conv_tile(up_g, carry_ref, cw_ref, cg, FF_CHUNK, row_ids) \
            + cb_ref[:, cg:cg + FF_CHUNK]
        act = (jax.nn.gelu(gat) * val).astype(BF16)
        part = _mm(act, wdn_ref[cv:cv + FF_CHUNK, :])
        if c == 0:
            acc_ref[...] = part
        else:
            acc_ref[...] += part
    o_ref[0] = _layer_norm(RES_ALPHA * x + gate * acc_ref[...], g_ref[...], b_ref[...])

    @pl.when(l == pl.num_programs(1) - 1)
    def _():
        st_ref[0] = carry_ref[SUBLANES - 2:SUBLANES, :]


def _ffn(x, mod, w_up, conv_w, conv_b, w_down, ln_g, ln_b):
    tiles = SEQ // ROW_TILE
    const = lambda n, l: (0, 0)
    return pl.pallas_call(
        _ffn_kernel,
        grid=(N_PROMPT, tiles),
        in_specs=[
            pl.BlockSpec((1, ROW_TILE, D_MODEL), lambda n, l: (n, l, 0)),
            pl.BlockSpec((1, 1, 3 * D_MODEL), lambda n, l: (n, 0, 0)),
            pl.BlockSpec((D_MODEL, 2 * D_FF), const),
            pl.BlockSpec((CONV_TAPS, 2 * D_FF), const),
            pl.BlockSpec((1, 2 * D_FF), const),
            pl.BlockSpec((D_FF, D_MODEL), const),
            pl.BlockSpec((1, D_MODEL), const),
            pl.BlockSpec((1, D_MODEL), const),
        ],
        out_specs=[
            pl.BlockSpec((1, ROW_TILE, D_MODEL), lambda n, l: (n, l, 0)),
            pl.BlockSpec((1, CONV_TAPS - 1, 2 * D_FF), lambda n, l: (n, 0, 0)),
        ],
        out_shape=[
            jax.ShapeDtypeStruct((N_PROMPT, SEQ, D_MODEL), F32),
            jax.ShapeDtypeStruct((N_PROMPT, CONV_TAPS - 1, 2 * D_FF), F32),
        ],
        scratch_shapes=[
            pltpu.VMEM((SUBLANES, 2 * D_FF), F32),
            pltpu.VMEM((ROW_TILE, D_MODEL), F32),
        ],
        compiler_params=_params(2),
        name="conv_ffn",
    )(x, mod, w_up, conv_w, conv_b, w_down, ln_g, ln_b)


def _sconv_kernel(x_ref, mod_ref, win_ref, cw_ref, wout_ref, g_ref, b_ref,
                  o_ref, st_ref, carry_ref, acc_ref):
    l = pl.program_id(1)

    @pl.when(l == 0)
    def _():
        carry_ref[...] = jnp.zeros_like(carry_ref)

    mod = mod_ref[0]
    shift, scale, gate = (mod[:, 0:D_MODEL], mod[:, D_MODEL:2 * D_MODEL],
                          mod[:, 2 * D_MODEL:3 * D_MODEL])
    x = x_ref[0]
    h = (x * (1.0 + scale) + shift).astype(BF16)
    row_ids = lax.broadcasted_iota(jnp.int32, (x.shape[0], FF_CHUNK), 0)
    for c in range(D_MODEL // FF_CHUNK):
        c0 = c * FF_CHUNK
        gate_b = _mm(h, win_ref[:, c0:c0 + FF_CHUNK])
        gate_c = _mm(h, win_ref[:, D_MODEL + c0:D_MODEL + c0 + FF_CHUNK])
        v = _mm(h, win_ref[:, 2 * D_MODEL + c0:2 * D_MODEL + c0 + FF_CHUNK])
        conv = _causal_conv_tile(gate_c * v, carry_ref, cw_ref, c0, FF_CHUNK, row_ids)
        part = _mm((gate_b * conv).astype(BF16), wout_ref[c0:c0 + FF_CHUNK, :])
        if c == 0:
            acc_ref[...] = part
        else:
            acc_ref[...] += part
    o_ref[0] = _layer_norm(RES_ALPHA * x + gate * acc_ref[...], g_ref[...], b_ref[...])

    @pl.when(l == pl.num_programs(1) - 1)
    def _():
        st_ref[0] = carry_ref[SUBLANES - 2:SUBLANES, :]


def _sconv(x, mod, w_in, conv_w, w_out, ln_g, ln_b):
    tiles = SEQ // ROW_TILE
    const = lambda n, l: (0, 0)
    return pl.pallas_call(
        _sconv_kernel,
        grid=(N_PROMPT, tiles),
        in_specs=[
            pl.BlockSpec((1, ROW_TILE, D_MODEL), lambda n, l: (n, l, 0)),
            pl.BlockSpec((1, 1, 3 * D_MODEL), lambda n, l: (n, 0, 0)),
            pl.BlockSpec((D_MODEL, 3 * D_MODEL), const),
            pl.BlockSpec((CONV_TAPS, D_MODEL), const),
            pl.BlockSpec((D_MODEL, D_MODEL), const),
            pl.BlockSpec((1, D_MODEL), const),
            pl.BlockSpec((1, D_MODEL), const),
        ],
        out_specs=[
            pl.BlockSpec((1, ROW_TILE, D_MODEL), lambda n, l: (n, l, 0)),
            pl.BlockSpec((1, CONV_TAPS - 1, D_MODEL), lambda n, l: (n, 0, 0)),
        ],
        out_shape=[
            jax.ShapeDtypeStruct((N_PROMPT, SEQ, D_MODEL), F32),
            jax.ShapeDtypeStruct((N_PROMPT, CONV_TAPS - 1, D_MODEL), F32),
        ],
        scratch_shapes=[
            pltpu.VMEM((SUBLANES, D_MODEL), F32),
            pltpu.VMEM((ROW_TILE, D_MODEL), F32),
        ],
        compiler_params=_params(2),
        name="short_conv",
    )(x, mod, w_in, conv_w, w_out, ln_g, ln_b)


def _s5_sample_kernel(x_ref, mod_ref, win_ref, e0_ref, f0_ref, a1_ref, hre_ref, him_ref,
                      d_ref, wglu_ref, g_ref, b_ref, o_ref, sre_ref, sim_ref):
    mod = mod_ref[...]
    shift, scale, gate = (mod[:, 0:D_MODEL], mod[:, D_MODEL:2 * D_MODEL],
                          mod[:, 2 * D_MODEL:3 * D_MODEL])
    x = x_ref[...]
    h = (x * (1.0 + scale) + shift).astype(BF16)
    u = _mm(h, win_ref[...])
    ys = []
    for j in range(S5_NJ):
        bu = _mm(u[:, j * LANES:(j + 1) * LANES].astype(BF16), e0_ref[j])
        ar = a1_ref[j][:, 0:S5_SW]
        ai = a1_ref[j][:, S5_SW:2 * S5_SW]
        hre = hre_ref[:, j * S5_SW:(j + 1) * S5_SW]
        him = him_ref[:, j * S5_SW:(j + 1) * S5_SW]
        nre = ar * hre - ai * him + bu[:, 0:S5_SW]
        nim = ar * him + ai * hre + bu[:, S5_SW:2 * S5_SW]
        sre_ref[:, j * S5_SW:(j + 1) * S5_SW] = nre
        sim_ref[:, j * S5_SW:(j + 1) * S5_SW] = nim
        ys.append(_mm(jnp.concatenate([nre, nim], axis=1).astype(BF16), f0_ref[j]))
    y = jnp.concatenate(ys, axis=1) + d_ref[...] * u
    z = _mm(jax.nn.gelu(y).astype(BF16), wglu_ref[...])
    m = z[:, 0:D_MODEL] * _sigmoid(z[:, D_MODEL:2 * D_MODEL])
    o_ref[...] = _layer_norm(RES_ALPHA * x + gate * m, g_ref[...], b_ref[...])


def _s5_sample(x, mod, w_in, e0, f0, a1, h_re, h_im, d_skip, w_glu, ln_g, ln_b):
    n = x.shape[0]
    st = jax.ShapeDtypeStruct((n, S5_GROUPS * S5_STATE), F32)
    return pl.pallas_call(
        _s5_sample_kernel,
        out_shape=[jax.ShapeDtypeStruct((n, D_MODEL), F32), st, st],
        compiler_params=_params(0),
        name="s5_sample",
    )(x, mod, w_in, e0, f0, a1, h_re, h_im, d_skip, w_glu, ln_g, ln_b)


def _ffn_sample_kernel(x_ref, mod_ref, wv_ref, wg_ref, cwv_ref, cwg_ref, cbv_ref, cbg_ref,
                       b0v_ref, b0g_ref, b1v_ref, b1g_ref, wdn_ref, g_ref, b_ref,
                       o_ref, upv_ref, upg_ref, acc_ref):
    c = pl.program_id(0)
    mod = mod_ref[...]
    shift, scale, gate = (mod[:, 0:D_MODEL], mod[:, D_MODEL:2 * D_MODEL],
                          mod[:, 2 * D_MODEL:3 * D_MODEL])
    x = x_ref[...]
    h = (x * (1.0 + scale) + shift).astype(BF16)
    up_v = _mm(h, wv_ref[...])
    up_g = _mm(h, wg_ref[...])
    upv_ref[...] = up_v
    upg_ref[...] = up_g
    cwv, cwg = cwv_ref[...], cwg_ref[...]
    val = cwv[0:1] * b0v_ref[...] + cwv[1:2] * b1v_ref[...] + cwv[2:3] * up_v + cbv_ref[...]
    gat = cwg[0:1] * b0g_ref[...] + cwg[1:2] * b1g_ref[...] + cwg[2:3] * up_g + cbg_ref[...]
    part = _mm((jax.nn.gelu(gat) * val).astype(BF16), wdn_ref[...])

    @pl.when(c == 0)
    def _():
        acc_ref[...] = part

    @pl.when(c > 0)
    def _():
        acc_ref[...] += part

    @pl.when(c == pl.num_programs(0) - 1)
    def _():
        o_ref[...] = _layer_norm(RES_ALPHA * x + gate * acc_ref[...], g_ref[...], b_ref[...])


def _ffn_sample(x, mod, w_up, conv_w, conv_b, buf, w_down, ln_g, ln_b):
    n = x.shape[0]
    nc = D_FF // FF_CHUNK
    full = lambda c: (0, 0)
    col = lambda off: (lambda c: (0, off + c))
    cb = conv_b.reshape(1, 2 * D_FF)
    return pl.pallas_call(
        _ffn_sample_kernel,
        grid=(nc,),
        in_specs=[
            pl.BlockSpec((n, D_MODEL), full),
            pl.BlockSpec((n, 3 * D_MODEL), full),
            pl.BlockSpec((D_MODEL, FF_CHUNK), col(0)),
            pl.BlockSpec((D_MODEL, FF_CHUNK), col(nc)),
            pl.BlockSpec((CONV_TAPS, FF_CHUNK), col(0)),
            pl.BlockSpec((CONV_TAPS, FF_CHUNK), col(nc)),
            pl.BlockSpec((1, FF_CHUNK), col(0)),
            pl.BlockSpec((1, FF_CHUNK), col(nc)),
            pl.BlockSpec((n, FF_CHUNK), col(0)),
            pl.BlockSpec((n, FF_CHUNK), col(nc)),
            pl.BlockSpec((n, FF_CHUNK), col(2 * nc)),
            pl.BlockSpec((n, FF_CHUNK), col(3 * nc)),
            pl.BlockSpec((FF_CHUNK, D_MODEL), lambda c: (c, 0)),
            pl.BlockSpec((1, D_MODEL), full),
            pl.BlockSpec((1, D_MODEL), full),
        ],
        out_specs=[
            pl.BlockSpec((n, D_MODEL), full),
            pl.BlockSpec((n, FF_CHUNK), col(0)),
            pl.BlockSpec((n, FF_CHUNK), col(0)),
        ],
        out_shape=[
            jax.ShapeDtypeStruct((n, D_MODEL), F32),
            jax.ShapeDtypeStruct((n, D_FF), F32),
            jax.ShapeDtypeStruct((n, D_FF), F32),
        ],
        scratch_shapes=[pltpu.VMEM((n, D_MODEL), F32)],
        compiler_params=_params(1),
        name="conv_ffn_sample",
    )(x, mod, w_up, w_up, conv_w, conv_w, cb, cb, buf, buf, buf, buf, w_down, ln_g, ln_b)


def _sconv_sample_kernel(x_ref, mod_ref, win_ref, cw_ref, buf_ref, wout_ref, g_ref, b_ref,
                         o_ref, p_ref):
    mod = mod_ref[...]
    shift, scale, gate = (mod[:, 0:D_MODEL], mod[:, D_MODEL:2 * D_MODEL],
                          mod[:, 2 * D_MODEL:3 * D_MODEL])
    x = x_ref[...]
    h = (x * (1.0 + scale) + shift).astype(BF16)
    z = _mm(h, win_ref[...])
    p = z[:, D_MODEL:2 * D_MODEL] * z[:, 2 * D_MODEL:3 * D_MODEL]
    p_ref[...] = p
    cw = cw_ref[...]
    conv = cw[0:1] * buf_ref[:, 0:D_MODEL] + cw[1:2] * buf_ref[:, D_MODEL:2 * D_MODEL] + cw[2:3] * p
    m = _mm((z[:, 0:D_MODEL] * conv).astype(BF16), wout_ref[...])
    o_ref[...] = _layer_norm(RES_ALPHA * x + gate * m, g_ref[...], b_ref[...])


def _sconv_sample(x, mod, w_in, conv_w, buf, w_out, ln_g, ln_b):
    n = x.shape[0]
    out = jax.ShapeDtypeStruct((n, D_MODEL), F32)
    return pl.pallas_call(
        _sconv_sample_kernel,
        out_shape=[out, out],
        compiler_params=_params(0),
        name="short_conv_sample",
    )(x, mod, w_in, conv_w, buf, w_out, ln_g, ln_b)


def kernel(x_prompt, x_sample, c_prompt, c_sample, state_s5_re, state_s5_im, state_conv, state_ffn, w_ada, b_ada, s5_w_in, s5_lam_re, s5_lam_im, s5_log_dt, s5_b_re, s5_b_im, s5_c_re, s5_c_im, s5_d, s5_w_glu, sc_w_in, sc_conv_w, sc_w_out, ffn_w_up, ffn_conv_w, ffn_conv_b, ffn_w_down, ln_g, ln_b):
    mods = _ada(jnp.concatenate([c_prompt, c_sample], axis=0), w_ada, b_ada)

    def mod_slices(layer, sub):
        base = (layer * 2 + sub) * 3 * D_MODEL
        m = mods[:, base:base + 3 * D_MODEL]
        return m[:N_PROMPT].reshape(N_PROMPT, 1, 3 * D_MODEL), m[N_PROMPT:]

    def ln_pair(layer, sub):
        return ln_g[layer, sub].reshape(1, D_MODEL), ln_b[layer, sub].reshape(1, D_MODEL)

    s5_w_in_h = s5_w_in[0].astype(BF16)
    s5_w_glu_h = s5_w_glu[0].astype(BF16)
    sc_w_in_h = sc_w_in[0].astype(BF16)
    sc_w_out_h = sc_w_out[0].astype(BF16)
    ffn_w_up_h = ffn_w_up.astype(BF16)
    ffn_w_down_h = ffn_w_down.astype(BF16)

    m_mat, e_mat, f_mat, f0_mat, a_1, a_t = _s5_prep(
        s5_lam_re[0], s5_lam_im[0], s5_log_dt[0], s5_b_re[0], s5_b_im[0], s5_c_re[0], s5_c_im[0])
    d_skip = s5_d[0].reshape(1, D_MODEL)
    d_tiled = jnp.tile(s5_d[0].reshape(S5_NJ, 1, LANES), (1, 1, S5_T))
    e0_mat = e_mat[:, (S5_T - 1) * LANES:S5_T * LANES, :]

    mp, ms = mod_slices(0, 0)
    g, b = ln_pair(0, 0)
    u = _s5_in(x_prompt, mp, s5_w_in_h)
    y, st = _s5_scan(u, m_mat, e_mat, f_mat, a_t, d_tiled)
    xp = _s5_out(y, x_prompt, mp, s5_w_glu_h, g, b)
    st = st.reshape(S5_NJ, N_PROMPT, 2, S5_GT, S5_STATE).transpose(2, 1, 0, 3, 4)
    st = st.reshape(2, 1, N_PROMPT, S5_GROUPS, S5_STATE)
    s5_re_p, s5_im_p = st[0], st[1]

    xs, sre, sim = _s5_sample(
        x_sample.reshape(N_SAMPLE, D_MODEL), ms, s5_w_in_h, e0_mat, f0_mat, a_1,
        state_s5_re[0].reshape(N_SAMPLE, S5_GROUPS * S5_STATE),
        state_s5_im[0].reshape(N_SAMPLE, S5_GROUPS * S5_STATE),
        d_skip, s5_w_glu_h, g, b)
    s5_re_s = sre.reshape(1, N_SAMPLE, S5_GROUPS, S5_STATE)
    s5_im_s = sim.reshape(1, N_SAMPLE, S5_GROUPS, S5_STATE)

    ffn_p, ffn_s = [], []

    def ffn_block(layer, xp, xs):
        mp, ms = mod_slices(layer, 1)
        g, b = ln_pair(layer, 1)
        xp, stp = _ffn(xp, mp, ffn_w_up_h[layer], ffn_conv_w[layer],
                       ffn_conv_b[layer].reshape(1, 2 * D_FF), ffn_w_down_h[layer], g, b)
        buf = state_ffn[layer]
        xs, up_v, up_g = _ffn_sample(
            xs, ms, ffn_w_up_h[layer], ffn_conv_w[layer], ffn_conv_b[layer],
            buf.reshape(N_SAMPLE, 2 * 2 * D_FF), ffn_w_down_h[layer], g, b)
        ffn_p.append(stp)
        ffn_s.append(jnp.stack([buf[:, 1, :], jnp.concatenate([up_v, up_g], axis=1)], axis=1))
        return xp, xs

    xp, xs = ffn_block(0, xp, xs)

    mp, ms = mod_slices(1, 0)
    g, b = ln_pair(1, 0)
    xp, conv_p = _sconv(xp, mp, sc_w_in_h, sc_conv_w[0], sc_w_out_h, g, b)
    cbuf = state_conv[0]
    xs, p_new = _sconv_sample(xs, ms, sc_w_in_h, sc_conv_w[0],
                              cbuf.reshape(N_SAMPLE, 2 * D_MODEL), sc_w_out_h, g, b)
    conv_s = jnp.stack([cbuf[:, 1, :], p_new], axis=1)

    xp, xs = ffn_block(1, xp, xs)

    return (xp, xs.reshape(N_SAMPLE, 1, D_MODEL),
            s5_re_p, s5_im_p, s5_re_s, s5_im_s,
            conv_p[None], conv_s[None],
            jnp.stack(ffn_p), jnp.stack(ffn_s))
```
